```python
import jax, jax.numpy as jnp
from jax import lax
import numpy as np

D_MODEL = 1024
BATCH = 4
SEQ = 4096
DEPTH = 4
DEC_BATCH = 32
DEC_SEQ = 8
PAST_LEN = 8192
PAGE_SIZE = 128

HEAD_DIM = 64
HEADS_PER_GROUP = 4
ATTN_GROUPS = ((128, 1), (512, 4), (2048, 16))
N_HEADS = HEADS_PER_GROUP * len(ATTN_GROUPS)
ATTN_WIDTH = N_HEADS * HEAD_DIM
MERGED_WIDTH = HEADS_PER_GROUP * HEAD_DIM
ROT_DIM = HEAD_DIM // 4
ROPE_THETA = 500000.0
POOL_WINDOWS = (2, 4, 8, 16)
POOL_GROUP_DIM = D_MODEL // 8
POOL_WIDTH = len(POOL_WINDOWS) * POOL_GROUP_DIM
POOL_PAD = max(POOL_WINDOWS) - 1
D_FF = 4 * D_MODEL
IN_WIDTH = 3 * ATTN_WIDTH + POOL_WIDTH + 2 * D_MODEL
Q_BLOCK = 128
RMS_EPS = 1e-6

kernel_name = "hybrid_dilated_pool_decoder_step"


def rmsnorm(x, g):
    xf = x.astype(jnp.float32)
    y = xf * lax.rsqrt(jnp.mean(xf * xf, axis=-1, keepdims=True) + RMS_EPS) * g.astype(jnp.float32)
    return y.astype(x.dtype)


def rope(x, pos):
    half = ROT_DIM // 2
    inv = ROPE_THETA ** (-jnp.arange(0, ROT_DIM, 2, dtype=jnp.float32) / ROT_DIM)
    ang = pos.astype(jnp.float32)[:, None] * inv[None, :]
    cos = jnp.cos(ang)[None, :, None, :]
    sin = jnp.sin(ang)[None, :, None, :]
    xr = x[..., :ROT_DIM].astype(jnp.float32)
    x1, x2 = xr[..., :half], xr[..., half:]
    rot = jnp.concatenate([x1 * cos - x2 * sin, x2 * cos + x1 * sin], axis=-1)
    return jnp.concatenate([rot.astype(x.dtype), x[..., ROT_DIM:]], axis=-1)


def dilated_group(q, k_all, v_all, q_idx, window, dilation):
    n_keys = window // dilation + 1
    idx = q_idx[:, None] - jnp.arange(n_keys)[None, :] * dilation
    valid = idx >= 0
    idx_c = jnp.maximum(idx, 0)
    kg = k_all[:, idx_c]
    vg = v_all[:, idx_c]
    s = jnp.einsum('bthd,btjhd->bhtj', q, kg).astype(jnp.float32) * (HEAD_DIM ** -0.5)
    s = jnp.where(valid[None, None], s, -jnp.inf)
    m = jnp.max(s, axis=-1, keepdims=True)
    p = jnp.exp(s - m)
    den = jnp.sum(p, axis=-1, keepdims=True)
    o = jnp.einsum('bhtj,btjhd->bthd', p / den, vg.astype(jnp.float32))
    lse = jnp.transpose((m + jnp.log(den))[..., 0], (0, 2, 1))
    return o, lse


def dilated_mixture(q, ks, vs, q_idxs):
    outs, lses = [], []
    for g, (w, d) in enumerate(ATTN_GROUPS):
        qg = q[:, :, g * HEADS_PER_GROUP:(g + 1) * HEADS_PER_GROUP]
        o, lse = dilated_group(qg, ks[g], vs[g], q_idxs[g], w, d)
        outs.append(o)
        lses.append(lse)
    wts = jax.nn.softmax(jnp.stack(lses, axis=0), axis=0)
    return jnp.sum(wts[..., None] * jnp.stack(outs, axis=0), axis=0)


def attention_branch(q, ks, vs, offsets):
    B, T = q.shape[0], q.shape[1]
    if T > Q_BLOCK and T % Q_BLOCK == 0:
        nb = T // Q_BLOCK
        qb = jnp.transpose(q.reshape(B, nb, Q_BLOCK, N_HEADS, HEAD_DIM), (1, 0, 2, 3, 4))
        ib = jnp.arange(T, dtype=jnp.int32).reshape(nb, Q_BLOCK)
        ob = lax.map(lambda a: dilated_mixture(a[0], ks, vs, [off + a[1] for off in offsets]), (qb, ib))
        o = jnp.transpose(ob, (1, 0, 2, 3, 4)).reshape(B, T, HEADS_PER_GROUP, HEAD_DIM)
    else:
        t = jnp.arange(T, dtype=jnp.int32)
        o = dilated_mixture(q, ks, vs, [off + t for off in offsets])
    return o.reshape(B, T, MERGED_WIDTH)


def pool_branch(u_ext, pos0, lin, scale):
    B = u_ext.shape[0]
    T = u_ext.shape[1] - POOL_PAD
    uf = u_ext.astype(jnp.float32)
    c = jnp.concatenate([jnp.zeros((B, 1, POOL_WIDTH), jnp.float32), jnp.cumsum(uf, axis=1)], axis=1)
    pos = pos0 + jnp.arange(T, dtype=jnp.int32)
    u_new = uf[:, POOL_PAD:]
    outs = []
    for g, w in enumerate(POOL_WINDOWS):
        sl = slice(g * POOL_GROUP_DIM, (g + 1) * POOL_GROUP_DIM)
        s = c[:, POOL_PAD + 1:POOL_PAD + 1 + T, sl] - c[:, POOL_PAD + 1 - w:POOL_PAD + 1 - w + T, sl]
        cnt = jnp.minimum(w, pos + 1).astype(jnp.float32)[None, :, None]
        outs.append(s / cnt - u_new[..., sl])
    p = jnp.stack(outs, axis=2)
    y = jnp.einsum('btgc,gcd->btgd', p, lin.astype(jnp.float32)).reshape(B, T, POOL_WIDTH)
    return (y * scale.astype(jnp.float32)).astype(u_ext.dtype)


def layer(x, pos, pos0, kv_bufs, pool_buf, norm1, w_in, w_pa, w_pb, pool_lin, pool_scale, w_o, norm2, w_up, w_down):
    B, T, _ = x.shape
    h = rmsnorm(x, norm1)
    proj = h @ w_in
    c1, c2, c3 = ATTN_WIDTH, 2 * ATTN_WIDTH, 3 * ATTN_WIDTH
    c4 = c3 + POOL_WIDTH
    c5 = c4 + D_MODEL
    q, k, v, u, ga, gb = jnp.split(proj, [c1, c2, c3, c4, c5], axis=-1)
    q = rope(q.reshape(B, T, N_HEADS, HEAD_DIM), pos)
    k = rope(k.reshape(B, T, N_HEADS, HEAD_DIM), pos)
    v = v.reshape(B, T, N_HEADS, HEAD_DIM)
    ks, vs, offs, new_kv = [], [], [], []
    for g, (w, d) in enumerate(ATTN_GROUPS):
        kg = k[:, :, g * HEADS_PER_GROUP:(g + 1) * HEADS_PER_GROUP]
        vg = v[:, :, g * HEADS_PER_GROUP:(g + 1) * HEADS_PER_GROUP]
        if kv_bufs is None:
            k_all, v_all, off, keep = kg, vg, 0, min(w, T)
        else:
            buf = kv_bufs[g]
            L = buf.shape[1]
            k_all = jnp.concatenate([buf[:, :, 0], kg], axis=1)
            v_all = jnp.concatenate([buf[:, :, 1], vg], axis=1)
            off, keep = L, L
        ks.append(k_all)
        vs.append(v_all)
        offs.append(off)
        new_kv.append(jnp.stack([k_all[:, -keep:], v_all[:, -keep:]], axis=2))
    o_a = attention_branch(q, ks, vs, offs).astype(x.dtype)
    if pool_buf is None:
        pool_buf = jnp.zeros((B, POOL_PAD, POOL_WIDTH), u.dtype)
    u_ext = jnp.concatenate([pool_buf, u], axis=1)
    new_pool = u_ext[:, -POOL_PAD:]
    o_b = pool_branch(u_ext, pos0, pool_lin, pool_scale)
    mixed = jax.nn.sigmoid(ga) * (o_a @ w_pa) + jax.nn.sigmoid(gb) * (o_b @ w_pb)
    x = x + mixed @ w_o
    h2 = rmsnorm(x, norm2)
    x = x + jnp.square(jax.nn.relu(h2 @ w_up)) @ w_down
    return x, new_kv, new_pool


def setup_inputs(seed: int = 0) -> dict:
    key = jax.random.key(seed)
    ks = jax.random.split(key, 20)
    f32 = jnp.float32
    nrm = lambda k, shape, s: jax.random.normal(k, shape, f32) * s
    win = [min(w, PAST_LEN) for (w, _) in ATTN_GROUPS]
    return {
        "x_prompt": nrm(ks[0], (BATCH, SEQ, D_MODEL), 1.0),
        "x_sample": nrm(ks[1], (DEC_BATCH, DEC_SEQ, D_MODEL), 1.0),
        "cache_kv_w128": nrm(ks[2], (DEPTH, DEC_BATCH, win[0], 2, HEADS_PER_GROUP, HEAD_DIM), 1.0),
        "cache_kv_w512": nrm(ks[3], (DEPTH, DEC_BATCH, win[1], 2, HEADS_PER_GROUP, HEAD_DIM), 1.0),
        "cache_kv_w2048": nrm(ks[4], (DEPTH, DEC_BATCH, win[2], 2, HEADS_PER_GROUP, HEAD_DIM), 1.0),
        "state_pool": nrm(ks[5], (DEPTH, DEC_BATCH, POOL_PAD, POOL_WIDTH), 1.0),
        "norm1": 1.0 + nrm(ks[6], (DEPTH, D_MODEL), 0.05),
        "w_in": nrm(ks[7], (DEPTH, D_MODEL, IN_WIDTH), D_MODEL ** -0.5),
        "w_pa": nrm(ks[8], (DEPTH, MERGED_WIDTH, D_MODEL), MERGED_WIDTH ** -0.5),
        "w_pb": nrm(ks[9], (DEPTH, POOL_WIDTH, D_MODEL), POOL_WIDTH ** -0.5),
        "pool_lin": nrm(ks[10], (DEPTH, len(POOL_WINDOWS), POOL_GROUP_DIM, POOL_GROUP_DIM), POOL_GROUP_DIM ** -0.5),
        "pool_scale": 1.0 + nrm(ks[11], (DEPTH, POOL_WIDTH), 0.05),
        "w_o": nrm(ks[12], (DEPTH, D_MODEL, D_MODEL), D_MODEL ** -0.5),
        "norm2": 1.0 + nrm(ks[13], (DEPTH, D_MODEL), 0.05),
        "w_up": nrm(ks[14], (DEPTH, D_MODEL, D_FF), D_MODEL ** -0.5),
        "w_down": nrm(ks[15], (DEPTH, D_FF, D_MODEL), D_FF ** -0.5),
        "final_norm": 1.0 + nrm(ks[16], (D_MODEL,), 0.05),
    }


def reference(x_prompt, x_sample, cache_kv_w128, cache_kv_w512, cache_kv_w2048, state_pool,
              norm1, w_in, w_pa, w_pb, pool_lin, pool_scale, w_o, norm2, w_up, w_down, final_norm):
    pos_p = jnp.arange(SEQ, dtype=jnp.int32)
    pos_s = PAST_LEN + jnp.arange(DEC_SEQ, dtype=jnp.int32)
    xp, xs = x_prompt, x_sample
    kvp = [[], [], []]
    kvs = [[], [], []]
    poolp, pools = [], []
    for l in range(DEPTH):
        wl = (norm1[l], w_in[l], w_pa[l], w_pb[l], pool_lin[l], pool_scale[l], w_o[l], norm2[l], w_up[l], w_down[l])
        xp, nkv_p, npool_p = layer(xp, pos_p, 0, None, None, *wl)
        bufs = [cache_kv_w128[l], cache_kv_w512[l], cache_kv_w2048[l]]
        xs, nkv_s, npool_s = layer(xs, pos_s, PAST_LEN, bufs, state_pool[l], *wl)
        for g in range(3):
            kvp[g].append(nkv_p[g])
            kvs[g].append(nkv_s[g])
        poolp.append(npool_p)
        pools.append(npool_s)
    y_prompt = rmsnorm(xp, final_norm)
    y_sample = rmsnorm(xs, final_norm)
    kv_w128_prompt = jnp.stack(kvp[0], axis=0)
    kv_w512_prompt = jnp.stack(kvp[1], axis=0)
    kv_w2048_prompt = jnp.stack(kvp[2], axis=0)
    pool_prompt = jnp.stack(poolp, axis=0)
    kv_w128_sample = jnp.stack(kvs[0], axis=0)
    kv_w512_sample = jnp.stack(kvs[1], axis=0)
    kv_w2048_sample = jnp.stack(kvs[2], axis=0)
    pool_sample = jnp.stack(pools, axis=0)
    return (y_prompt, y_sample, kv_w128_prompt, kv_w512_prompt, kv_w2048_prompt, pool_prompt,
            kv_w128_sample, kv_w512_sample, kv_w2048_sample, pool_sample)
```

```python
import functools

import numpy as np
import jax
import jax.numpy as jnp
from jax import lax
from jax.experimental import pallas as pl
from jax.experimental.pallas import tpu as pltpu

D_MODEL = 1024
HEAD_DIM = 64
HEADS_PER_GROUP = 4
ATTN_GROUPS = ((128, 1), (512, 4), (2048, 16))
N_GROUPS = len(ATTN_GROUPS)
GROUP_WIDTH = HEADS_PER_GROUP * HEAD_DIM
ATTN_WIDTH = N_GROUPS * GROUP_WIDTH
KV_WIDTH = 2 * ATTN_WIDTH
ROT_DIM = HEAD_DIM // 4
ROT_HALF = ROT_DIM // 2
ROPE_THETA = 500000.0
POOL_WINDOWS = (2, 4, 8, 16)
POOL_GROUP_DIM = D_MODEL // 8
POOL_WIDTH = len(POOL_WINDOWS) * POOL_GROUP_DIM
POOL_PAD = max(POOL_WINDOWS) - 1
D_FF = 4 * D_MODEL
PAST_LEN = 8192
RMS_EPS = 1e-6
QKVU_WIDTH = ATTN_WIDTH + KV_WIDTH + POOL_WIDTH
GATE_WIDTH = 2 * D_MODEL

LANES = 128
SUBLANES = 8
VMEM_LIMIT_BYTES = 56 * 1024 * 1024
KEYS_PER_QUERY_BLOCK = 128
Q_BLOCK = 128
TOKEN_TILE = 256
NEG_BIG = -1e30

assert all(w // d == KEYS_PER_QUERY_BLOCK for w, d in ATTN_GROUPS)


def _cparams(n_grid):
    return pltpu.CompilerParams(dimension_semantics=("parallel",) * n_grid,
                                vmem_limit_bytes=VMEM_LIMIT_BYTES)


def _resident(shape):
    nd = len(shape)
    return pl.BlockSpec(shape, lambda *_: (0,) * nd, pipeline_mode=pl.Buffered(1))


def _rmsnorm(x, g):
    return x * lax.rsqrt(jnp.mean(x * x, axis=-1, keepdims=True) + RMS_EPS) * g


def _head_lane_masks(rows):
    lane = lax.broadcasted_iota(jnp.int32, (rows, GROUP_WIDTH), 1)
    return [(lane >= h * HEAD_DIM) & (lane < (h + 1) * HEAD_DIM) for h in range(HEADS_PER_GROUP)]


def _rope_tables(pos):
    inv = ROPE_THETA ** (-jnp.arange(0, ROT_DIM, 2, dtype=jnp.float32) / ROT_DIM)
    ang = pos.astype(jnp.float32)[:, None] * inv[None, :]
    cos, sin = jnp.cos(ang), jnp.sin(ang)
    t = pos.shape[0]
    ones = jnp.ones((t, HEAD_DIM - ROT_DIM), jnp.float32)
    zeros = jnp.zeros((t, HEAD_DIM - ROT_DIM), jnp.float32)
    z8 = jnp.zeros((t, ROT_HALF), jnp.float32)
    cos_h = jnp.concatenate([cos, cos, ones], axis=1)
    up_h = jnp.concatenate([z8, sin, zeros], axis=1)
    dn_h = jnp.concatenate([-sin, z8, zeros], axis=1)
    rep = LANES // HEAD_DIM
    return jnp.tile(cos_h, (1, rep)), jnp.tile(up_h, (1, rep)), jnp.tile(dn_h, (1, rep))


def _inproj_kernel(x_ref, g_ref, w_ref, cos_ref, up_ref, dn_ref, q_ref, kv_ref, u_ref):
    h = _rmsnorm(x_ref[...], g_ref[...]).astype(jnp.bfloat16)
    cos, up, dn = cos_ref[...], up_ref[...], dn_ref[...]

    def rope(y):
        parts = []
        for c in range(GROUP_WIDTH // LANES):
            yc = y[:, c * LANES:(c + 1) * LANES]
            parts.append(yc * cos + pltpu.roll(yc, ROT_HALF, 1) * up
                         + pltpu.roll(yc, LANES - ROT_HALF, 1) * dn)
        return jnp.concatenate(parts, axis=1)

    def proj(c0):
        return jnp.dot(h, w_ref[:, c0:c0 + GROUP_WIDTH], preferred_element_type=jnp.float32)

    for g in range(N_GROUPS):
        c0 = g * GROUP_WIDTH
        q_ref[:, c0:c0 + GROUP_WIDTH] = (rope(proj(c0)) * (HEAD_DIM ** -0.5)).astype(q_ref.dtype)
    for g in range(N_GROUPS):
        c0 = 2 * g * GROUP_WIDTH
        kv_ref[:, c0:c0 + GROUP_WIDTH] = rope(proj(ATTN_WIDTH + c0))
        kv_ref[:, c0 + GROUP_WIDTH:c0 + 2 * GROUP_WIDTH] = proj(ATTN_WIDTH + c0 + GROUP_WIDTH)
    for c0 in range(0, POOL_WIDTH, GROUP_WIDTH):
        u_ref[:, c0:c0 + GROUP_WIDTH] = proj(ATTN_WIDTH + KV_WIDTH + c0)


def _inproj(x, norm1, w_qkvu, tables, rows_per_seq):
    n = x.shape[0]
    tm = min(TOKEN_TILE, n)
    assert n % tm == 0 and rows_per_seq % tm == 0
    tiles_per_seq = rows_per_seq // tm
    tok = lambda w: pl.BlockSpec((tm, w), lambda i: (i, 0))
    tab = pl.BlockSpec((tm, LANES), lambda i: (i % tiles_per_seq, 0))
    return pl.pallas_call(
        _inproj_kernel,
        grid=(n // tm,),
        in_specs=[tok(D_MODEL), _resident((1, D_MODEL)), _resident((D_MODEL, QKVU_WIDTH)), tab, tab, tab],
        out_specs=[tok(ATTN_WIDTH), tok(KV_WIDTH), tok(POOL_WIDTH)],
        out_shape=[jax.ShapeDtypeStruct((n, ATTN_WIDTH), jnp.bfloat16),
                   jax.ShapeDtypeStruct((n, KV_WIDTH), jnp.float32),
                   jax.ShapeDtypeStruct((n, POOL_WIDTH), jnp.float32)],
        compiler_params=_cparams(1),
        name="inproj",
    )(x, norm1, w_qkvu, *tables)


def _stacked_head_attention(q, k, v, bias, masks):
    r = q.shape[0]
    zero = jnp.zeros_like(q)
    q4 = jnp.concatenate([jnp.where(m, q, zero) for m in masks], axis=0).astype(jnp.bfloat16)
    s = lax.dot_general(q4, k, (((1,), (1,)), ((), ())), preferred_element_type=jnp.float32) + bias
    m = jnp.max(s, axis=-1, keepdims=True)
    p = jnp.exp(s - m)
    den = jnp.sum(p, axis=-1, keepdims=True)
    o4 = jnp.dot(p.astype(jnp.bfloat16), v, preferred_element_type=jnp.float32)
    o4 = o4 * (1.0 / den)
    lse4 = m + jnp.log(den)
    o = jnp.zeros((r, GROUP_WIDTH), jnp.float32)
    lse = jnp.zeros((r, GROUP_WIDTH), jnp.float32)
    for h, mk in enumerate(masks):
        o = jnp.where(mk, o4[h * r:(h + 1) * r], o)
        lse = jnp.where(mk, lse4[h * r:(h + 1) * r], lse)
    return o, lse


def _prompt_attn_kernel(units_per_residue, *refs):
    ins, outs = refs[:3 * N_GROUPS], refs[3 * N_GROUPS:]
    u = pl.program_id(1)
    masks = _head_lane_masks(Q_BLOCK)
    row = lax.broadcasted_iota(jnp.int32, (Q_BLOCK, 2 * Q_BLOCK), 0)
    col = lax.broadcasted_iota(jnp.int32, (Q_BLOCK, 2 * Q_BLOCK), 1)
    rel = col - row
    band = jnp.where((rel >= 0) & (rel <= KEYS_PER_QUERY_BLOCK), 0.0, NEG_BIG).astype(jnp.float32)
    band4 = jnp.concatenate([band] * HEADS_PER_GROUP, axis=0)
    halo_cols = jnp.concatenate([jnp.where(col < Q_BLOCK, NEG_BIG, 0.0).astype(jnp.float32)] * HEADS_PER_GROUP,
                                axis=0)
    for g in range(N_GROUPS):
        q_ref, kvh_ref, kvm_ref = ins[3 * g:3 * g + 3]
        o_ref, lse_ref = outs[2 * g:2 * g + 2]
        first = (u % units_per_residue[g]) == 0
        kv_all = jnp.concatenate([kvh_ref[0], kvm_ref[0]], axis=0).astype(jnp.bfloat16)
        unit_rows = kvm_ref.shape[1]
        for i in range(unit_rows // Q_BLOCK):
            r0 = i * Q_BLOCK
            kv_blk = kv_all[r0:r0 + 2 * Q_BLOCK]
            bias = band4
            if i == 0:
                bias = bias + jnp.where(first, halo_cols, 0.0)
            o, lse = _stacked_head_attention(q_ref[0, r0:r0 + Q_BLOCK, :], kv_blk[:, :GROUP_WIDTH],
                                             kv_blk[:, GROUP_WIDTH:], bias, masks)
            o_ref[0, r0:r0 + Q_BLOCK, :] = o.astype(o_ref.dtype)
            lse_ref[0, r0:r0 + Q_BLOCK, :] = lse


def _prompt_attention(q, kv, batch, seq):
    n_units = 16
    unit_rows = seq // n_units
    assert seq % n_units == 0 and unit_rows % Q_BLOCK == 0
    blocks_per_unit = unit_rows // Q_BLOCK
    in_specs, out_specs, out_shape, operands, upr = [], [], [], [], []
    for g, (_, d) in enumerate(ATTN_GROUPS):
        assert n_units % d == 0
        cpr = n_units // d
        upr.append(cpr)
        rows = seq // d
        qv = q.reshape(batch, rows, d * ATTN_WIDTH)
        kvv = kv.reshape(batch, rows, d * KV_WIDTH)

        def main_map(b, u, g=g, cpr=cpr):
            return (b, u % cpr, (u // cpr) * N_GROUPS + g)

        def halo_map(b, u, g=g, cpr=cpr):
            return (b, jnp.maximum((u % cpr) * blocks_per_unit - 1, 0), (u // cpr) * N_GROUPS + g)

        def out_map(b, u, cpr=cpr):
            return (b, u % cpr, u // cpr)

        in_specs += [pl.BlockSpec((1, unit_rows, GROUP_WIDTH), main_map),
                     pl.BlockSpec((1, Q_BLOCK, 2 * GROUP_WIDTH), halo_map),
                     pl.BlockSpec((1, unit_rows, 2 * GROUP_WIDTH), main_map)]
        operands += [qv, kvv, kvv]
        out_specs += [pl.BlockSpec((1, unit_rows, GROUP_WIDTH), out_map)] * 2
        out_shape += [jax.ShapeDtypeStruct((batch, rows, d * GROUP_WIDTH), jnp.bfloat16),
                      jax.ShapeDtypeStruct((batch, rows, d * GROUP_WIDTH), jnp.float32)]
    res = pl.pallas_call(
        functools.partial(_prompt_attn_kernel, tuple(upr)),
        grid=(batch, n_units),
        in_specs=in_specs, out_specs=out_specs, out_shape=out_shape,
        compiler_params=_cparams(2),
        name="prompt_attn",
    )(*operands)
    n = batch * seq
    return [r.reshape(n, GROUP_WIDTH) for r in res]


def _sample_bias(g, dec_seq):
    w, d = ATTN_GROUPS[g]
    cache_len = w
    if g == N_GROUPS - 1:
        n_cache = KEYS_PER_QUERY_BLOCK * dec_seq
        cc = np.arange(n_cache)
        cache_row = d * (cc % KEYS_PER_QUERY_BLOCK) + cc // KEYS_PER_QUERY_BLOCK
    else:
        cache_row = np.arange(cache_len)
    t = np.arange(dec_seq)[:, None]
    new_t = np.arange(LANES)[None, :]
    diff_new = t - new_t
    ok_new = (new_t < dec_seq) & (diff_new >= 0) & (diff_new % d == 0) & (diff_new <= w)
    diff_c = t + cache_len - cache_row[None, :]
    ok_c = (diff_c >= 0) & (diff_c % d == 0) & (diff_c <= w)
    ok = np.concatenate([ok_new, ok_c], axis=1)
    bias = np.where(ok, 0.0, NEG_BIG).astype(np.float32)
    return np.tile(bias, (HEADS_PER_GROUP, 1))


def _pool_counts(rows, pos0, w):
    pos = pos0 + lax.broadcasted_iota(jnp.int32, (rows, 1), 0)
    return jnp.minimum(w, pos + 1).astype(jnp.float32)


def _sample_attn_kernel(dec_seq, q_ref, kvn_ref, c0_ref, c1_ref, c2_ref, b0_ref, b1_ref, b2_ref,
                        state_ref, u_ref, *rest):
    outs, ext_ref = rest[:2 * N_GROUPS + 1], rest[2 * N_GROUPS + 1]
    masks = _head_lane_masks(dec_seq)
    q = q_ref[0]
    kvn = kvn_ref[0]
    pad = jnp.zeros((LANES - dec_seq, 2 * GROUP_WIDTH), jnp.float32)
    caches = (c0_ref, c1_ref, c2_ref)
    biases = (b0_ref, b1_ref, b2_ref)
    for g in range(N_GROUPS):
        c_ref = caches[g]
        if g == N_GROUPS - 1:
            cache = jnp.concatenate(
                [c_ref[0, 0, :, t * 2 * GROUP_WIDTH:(t + 1) * 2 * GROUP_WIDTH] for t in range(dec_seq)], axis=0)
        else:
            cache = c_ref[0, 0]
        new = kvn[:, g * 2 * GROUP_WIDTH:(g + 1) * 2 * GROUP_WIDTH]
        kv_all = jnp.concatenate([new, pad, cache], axis=0).astype(jnp.bfloat16)
        o, lse = _stacked_head_attention(q[:, g * GROUP_WIDTH:(g + 1) * GROUP_WIDTH],
                                         kv_all[:, :GROUP_WIDTH], kv_all[:, GROUP_WIDTH:], biases[g][...], masks)
        outs[2 * g][0] = o.astype(outs[2 * g].dtype)
        outs[2 * g + 1][0] = lse
    p_ref = outs[2 * N_GROUPS]
    hist = POOL_PAD + 1
    ext_ref[0:1, :] = jnp.zeros((1, POOL_WIDTH), jnp.float32)
    ext_ref[1:hist, :] = state_ref[0, 0]
    ext_ref[hist:hist + dec_seq, :] = u_ref[0]
    for gi, w in enumerate(POOL_WINDOWS):
        sl = slice(gi * POOL_GROUP_DIM, (gi + 1) * POOL_GROUP_DIM)
        s = ext_ref[hist:hist + dec_seq, sl]
        for k in range(1, w):
            s = s + ext_ref[hist - k:hist - k + dec_seq, sl]
        p_ref[0, :, sl] = s / _pool_counts(dec_seq, PAST_LEN, w) - u_ref[0, :, sl]


def _sample_attention(layer, q, kv_new, caches, state_pool, u, dec_batch, dec_seq):
    assert dec_seq == SUBLANES and dec_seq <= ATTN_GROUPS[-1][1] // 2
    depth = caches[0].shape[0]
    w2, d2 = ATTN_GROUPS[-1]
    c0 = caches[0].reshape(depth, dec_batch, ATTN_GROUPS[0][0], 2 * GROUP_WIDTH)
    c1 = caches[1].reshape(depth, dec_batch, ATTN_GROUPS[1][0], 2 * GROUP_WIDTH)
    c2 = caches[2].reshape(depth, dec_batch, w2 // d2, d2 * 2 * GROUP_WIDTH)
    biases = [jnp.asarray(_sample_bias(g, dec_seq)) for g in range(N_GROUPS)]
    per_b3 = lambda w: pl.BlockSpec((1, dec_seq, w), lambda b: (b, 0, 0))
    cache_spec = lambda rows, w: pl.BlockSpec((1, 1, rows, w), lambda b: (layer, b, 0, 0))
    out_specs, out_shape = [], []
    for _ in range(N_GROUPS):
        out_specs += [per_b3(GROUP_WIDTH), per_b3(GROUP_WIDTH)]
        out_shape += [jax.ShapeDtypeStruct((dec_batch, dec_seq, GROUP_WIDTH), jnp.float32)] * 2
    out_specs.append(per_b3(POOL_WIDTH))
    out_shape.append(jax.ShapeDtypeStruct((dec_batch, dec_seq, POOL_WIDTH), jnp.float32))
    res = pl.pallas_call(
        functools.partial(_sample_attn_kernel, dec_seq),
        grid=(dec_batch,),
        in_specs=[per_b3(ATTN_WIDTH), per_b3(KV_WIDTH),
                  cache_spec(c0.shape[2], 2 * GROUP_WIDTH), cache_spec(c1.shape[2], 2 * GROUP_WIDTH),
                  cache_spec(c2.shape[2], dec_seq * 2 * GROUP_WIDTH),
                  _resident(biases[0].shape), _resident(biases[1].shape), _resident(biases[2].shape),
                  pl.BlockSpec((1, 1, POOL_PAD, POOL_WIDTH), lambda b: (layer, b, 0, 0)),
                  per_b3(POOL_WIDTH)],
        out_specs=out_specs, out_shape=out_shape,
        scratch_shapes=[pltpu.VMEM((POOL_PAD + 1 + dec_seq, POOL_WIDTH), jnp.float32)],
        compiler_params=_cparams(1),
        name="sample_attn",
    )(q, kv_new, c0, c1, c2, *biases, state_pool, u)
    n = dec_batch * dec_seq
    return [r.reshape(n, r.shape[-1]) for r in res]


POOL_HALO = 2 * SUBLANES
POOL_LEAD = 3 * SUBLANES


def _pool_features_tile(u_ref, halo_ref, buf_a, buf_b, tile_in_seq, tm):
    zeros8 = jnp.zeros((SUBLANES, POOL_WIDTH), jnp.float32)
    buf_a[0:SUBLANES, :] = zeros8
    buf_b[0:SUBLANES, :] = zeros8
    buf_a[SUBLANES:POOL_LEAD, :] = jnp.where(tile_in_seq == 0, 0.0, halo_ref[...])
    buf_a[POOL_LEAD:, :] = u_ref[...]
    n = tm + POOL_HALO
    src, dst = buf_a, buf_b
    for level, shift in enumerate((1, 2, 4, 8)):
        lanes = slice(level * POOL_GROUP_DIM, POOL_WIDTH)
        dst[SUBLANES:SUBLANES + n, lanes] = (src[SUBLANES:SUBLANES + n, lanes]
                                             + src[SUBLANES - shift:SUBLANES - shift + n, lanes])
        src, dst = dst, src
    feats = []
    pos0 = tile_in_seq * tm
    for gi, w in enumerate(POOL_WINDOWS):
        sl = slice(gi * POOL_GROUP_DIM, (gi + 1) * POOL_GROUP_DIM)
        holder = buf_b if gi % 2 == 0 else buf_a
        s = holder[POOL_LEAD:POOL_LEAD + tm, sl]
        feats.append(s / _pool_counts(tm, pos0, w) - u_ref[:, sl])
    return feats


def _post_kernel(pool_from_halo, final, tiles_per_seq, *refs):
    it = iter(refs)
    x_ref = next(it)
    og = [next(it) for _ in range(N_GROUPS)]
    lg = [next(it) for _ in range(N_GROUPS)]
    u_ref = next(it)
    halo_ref = next(it) if pool_from_halo else None
    (n1_ref, wg_ref, wpa_ref, wpb_ref, lin_ref, ps_ref, wo_ref, n2_ref, wup_ref, wdn_ref) = [next(it) for _ in range(10)]
    fn_ref = next(it) if final else None
    out_ref = next(it)
    bufs = list(it)
    tm = x_ref.shape[0]
    bf = jnp.bfloat16
    dot = functools.partial(jnp.dot, preferred_element_type=jnp.float32)

    x = x_ref[...]
    lses = [r[...] for r in lg]
    mx = jnp.maximum(jnp.maximum(lses[0], lses[1]), lses[2])
    es = [jnp.exp(l - mx) for l in lses]
    num = es[0] * og[0][...].astype(jnp.float32)
    for g in range(1, N_GROUPS):
        num = num + es[g] * og[g][...].astype(jnp.float32)
    o_a = (num / (es[0] + es[1] + es[2])).astype(bf)
    if pool_from_halo:
        feats = _pool_features_tile(u_ref, halo_ref, bufs[0], bufs[1], pl.program_id(0) % tiles_per_seq, tm)
    else:
        feats = [u_ref[:, gi * POOL_GROUP_DIM:(gi + 1) * POOL_GROUP_DIM] for gi in range(len(POOL_WINDOWS))]
    o_b = jnp.concatenate([dot(f.astype(bf), lin_ref[gi]) for gi, f in enumerate(feats)], axis=1)
    o_b = (o_b * ps_ref[...]).astype(bf)
    h = _rmsnorm(x, n1_ref[...]).astype(bf)
    gates = 1.0 / (1.0 + jnp.exp(-dot(h, wg_ref[...])))
    mixed = gates[:, :D_MODEL] * dot(o_a, wpa_ref[...]) + gates[:, D_MODEL:] * dot(o_b, wpb_ref[...])
    x = x + dot(mixed.astype(bf), wo_ref[...])
    h2 = _rmsnorm(x, n2_ref[...]).astype(bf)
    act = jnp.square(jnp.maximum(dot(h2, wup_ref[...]), 0.0)).astype(bf)
    x = x + dot(act, wdn_ref[...])
    if final:
        x = _rmsnorm(x, fn_ref[...])
    out_ref[...] = x


def _post(x, o_groups, lse_groups, u_or_feats, weights, final_norm, rows_per_seq, pool_from_halo):
    n = x.shape[0]
    tm = min(TOKEN_TILE, n)
    assert n % tm == 0 and tm % POOL_HALO == 0
    tiles_per_seq = max(rows_per_seq // tm, 1)
    final = final_norm is not None
    tok = lambda w: pl.BlockSpec((tm, w), lambda i: (i, 0))
    in_specs = [tok(D_MODEL)] + [tok(GROUP_WIDTH)] * (2 * N_GROUPS) + [tok(POOL_WIDTH)]
    operands = [x, *o_groups, *lse_groups, u_or_feats]
    scratch = []
    if pool_from_halo:
        assert rows_per_seq % tm == 0
        ratio = tm // POOL_HALO
        in_specs.append(pl.BlockSpec((POOL_HALO, POOL_WIDTH), lambda i: (jnp.maximum(i * ratio - 1, 0), 0)))
        operands.append(u_or_feats)
        scratch = [pltpu.VMEM((POOL_LEAD + tm, POOL_WIDTH), jnp.float32)] * 2
    in_specs += [_resident(w.shape) for w in weights]
    operands += list(weights)
    if final:
        in_specs.append(_resident(final_norm.shape))
        operands.append(final_norm)
    return pl.pallas_call(
        functools.partial(_post_kernel, pool_from_halo, final, tiles_per_seq),
        grid=(n // tm,),
        in_specs=in_specs,
        out_specs=tok(D_MODEL),
        out_shape=jax.ShapeDtypeStruct((n, D_MODEL), jnp.float32),
        scratch_shapes=scratch,
        compiler_params=_cparams(1),
        name="post_final" if final else "post",
    )(*operands)


def _cache_update_kernel(dec_seq, kvn_ref, c0_ref, c1_ref, c2_ref, o0_ref, o1_ref, o2_ref):
    for g, (c_ref, o_ref) in enumerate(((c0_ref, o0_ref), (c1_ref, o1_ref), (c2_ref, o2_ref))):
        keep = c_ref.shape[2] - dec_seq
        o_ref[0, 0, 0:keep, :] = c_ref[0, 0, dec_seq:, :]
        o_ref[0, 0, keep:, :] = kvn_ref[0, 0, :, g * 2 * GROUP_WIDTH:(g + 1) * 2 * GROUP_WIDTH]


def _cache_update(kv_new, caches):
    depth, dec_batch, dec_seq, _ = kv_new.shape
    flat = [c.reshape(depth, dec_batch, c.shape[2], 2 * GROUP_WIDTH) for c in caches]
    spec = lambda rows, w: pl.BlockSpec((1, 1, rows, w), lambda l, b: (l, b, 0, 0))
    cache_specs = [spec(c.shape[2], 2 * GROUP_WIDTH) for c in flat]
    res = pl.pallas_call(
        functools.partial(_cache_update_kernel, dec_seq),
        grid=(depth, dec_batch),
        in_specs=[spec(dec_seq, KV_WIDTH)] + cache_specs,
        out_specs=cache_specs,
        out_shape=[jax.ShapeDtypeStruct(c.shape, c.dtype) for c in flat],
        compiler_params=_cparams(2),
        name="cache_update",
    )(kv_new, *flat)
    return [r.reshape(c.shape) for r, c in zip(res, caches)]


def _qkvu_weights(w_in):
    parts = [w_in[..., :ATTN_WIDTH]]
    for g in range(N_GROUPS):
        parts.append(w_in[..., ATTN_WIDTH + g * GROUP_WIDTH:ATTN_WIDTH + (g + 1) * GROUP_WIDTH])
        parts.append(w_in[..., 2 * ATTN_WIDTH + g * GROUP_WIDTH:2 * ATTN_WIDTH + (g + 1) * GROUP_WIDTH])
    parts.append(w_in[..., 3 * ATTN_WIDTH:3 * ATTN_WIDTH + POOL_WIDTH])
    return jnp.concatenate([p.astype(jnp.bfloat16) for p in parts], axis=-1)


def kernel(x_prompt, x_sample, cache_kv_w128, cache_kv_w512, cache_kv_w2048, state_pool, norm1, w_in, w_pa, w_pb,
           pool_lin, pool_scale, w_o, norm2, w_up, w_down, final_norm):
    batch, seq, _ = x_prompt.shape
    dec_batch, dec_seq, _ = x_sample.shape
    depth = w_in.shape[0]
    bf = jnp.bfloat16
    caches = (cache_kv_w128, cache_kv_w512, cache_kv_w2048)

    w_qkvu = _qkvu_weights(w_in)
    w_gate = w_in[:, :, 3 * ATTN_WIDTH + POOL_WIDTH:].astype(bf)
    w_pa_b, w_pb_b, lin_b, w_o_b, w_up_b, w_dn_b = (w.astype(bf) for w in (w_pa, w_pb, pool_lin, w_o, w_up, w_down))
    norm1_r, norm2_r, scale_r = norm1[:, None, :], norm2[:, None, :], pool_scale[:, None, :]
    fnorm = final_norm[None, :]

    tab_p = _rope_tables(jnp.arange(seq, dtype=jnp.int32))
    tab_s = tuple(jnp.tile(t, (dec_batch, 1)) for t in _rope_tables(PAST_LEN + jnp.arange(dec_seq, dtype=jnp.int32)))

    xp = x_prompt.reshape(batch * seq, D_MODEL)
    xs = x_sample.reshape(dec_batch * dec_seq, D_MODEL)
    n_s = dec_batch * dec_seq
    kv_p, u_p, kv_s, u_s = [], [], [], []
    for l in range(depth):
        weights = (norm1_r[l], w_gate[l], w_pa_b[l], w_pb_b[l], lin_b[l], scale_r[l], w_o_b[l], norm2_r[l],
                   w_up_b[l], w_dn_b[l])
        fin = fnorm if l == depth - 1 else None
        q, kv, u = _inproj(xp, norm1_r[l], w_qkvu[l], tab_p, seq)
        ol = _prompt_attention(q, kv, batch, seq)
        xp = _post(xp, ol[0::2], ol[1::2], u, weights, fin, seq, True)
        kv3 = kv.reshape(batch, seq, KV_WIDTH)
        kv_p.append([kv3[:, seq - min(w, seq):, g * 2 * GROUP_WIDTH:(g + 1) * 2 * GROUP_WIDTH]
                     for g, (w, _) in enumerate(ATTN_GROUPS)])
        u_p.append(u.reshape(batch, seq, POOL_WIDTH)[:, seq - POOL_PAD:])
        q, kv, u = _inproj(xs, norm1_r[l], w_qkvu[l], tab_s, n_s)
        kv3 = kv.reshape(dec_batch, dec_seq, KV_WIDTH)
        u3 = u.reshape(dec_batch, dec_seq, POOL_WIDTH)
        res = _sample_attention(l, q.astype(jnp.float32).reshape(dec_batch, dec_seq, ATTN_WIDTH), kv3, caches,
                                state_pool, u3, dec_batch, dec_seq)
        xs = _post(xs, res[0:6:2], res[1:6:2], res[6], weights, fin, n_s, False)
        kv_s.append(kv3)
        u_s.append(u3)

    y_prompt = xp.reshape(batch, seq, D_MODEL)
    y_sample = xs.reshape(dec_batch, dec_seq, D_MODEL)
    prompt_kv = []
    for g, (w, _) in enumerate(ATTN_GROUPS):
        sl = jnp.stack([per_layer[g] for per_layer in kv_p], axis=0)
        prompt_kv.append(sl.reshape(depth, batch, min(w, seq), 2, HEADS_PER_GROUP, HEAD_DIM))
    pool_prompt = jnp.stack(u_p, axis=0)
    sample_kv = _cache_update(jnp.stack(kv_s, axis=0), caches)
    pool_sample = jnp.concatenate([state_pool, jnp.stack(u_s, axis=0)], axis=2)[:, :, -POOL_PAD:]
    return (y_prompt, y_sample, *prompt_kv, pool_prompt, *sample_kv, pool_sample)
```

```python
import functools

import numpy as np
import jax
import jax.numpy as jnp
from jax import lax
from jax.experimental import pallas as pl
from jax.experimental.pallas import tpu as pltpu

D_MODEL = 1024
HEAD_DIM = 64
HEADS_PER_GROUP = 4
ATTN_GROUPS = ((128, 1), (512, 4), (2048, 16))
N_GROUPS = len(ATTN_GROUPS)
GROUP_WIDTH = HEADS_PER_GROUP * HEAD_DIM
KV_GROUP_WIDTH = 2 * GROUP_WIDTH
ATTN_WIDTH = N_GROUPS * GROUP_WIDTH
KV_WIDTH = 2 * ATTN_WIDTH
ROT_DIM = HEAD_DIM // 4
ROT_HALF = ROT_DIM // 2
ROPE_THETA = 500000.0
POOL_WINDOWS = (2, 4, 8, 16)
POOL_GROUP_DIM = D_MODEL // 8
POOL_WIDTH = len(POOL_WINDOWS) * POOL_GROUP_DIM
POOL_PAD = max(POOL_WINDOWS) - 1
D_FF = 4 * D_MODEL
PAST_LEN = 8192
RMS_EPS = 1e-6
QKVU_WIDTH = ATTN_WIDTH + KV_WIDTH + POOL_WIDTH

LANES = 128
SUBLANES = 8
VMEM_LIMIT_BYTES = 56 * 1024 * 1024
KEYS_BACK = 128
Q_BLOCK = 128
TOKEN_TILE = 256
MAX_DILATION = max(d for _, d in ATTN_GROUPS)
SUPER_TILE = MAX_DILATION * Q_BLOCK
BLOCKS_PER_SUPER = SUPER_TILE // Q_BLOCK
MERGE_ROWS = 256
NEG_BIG = -1e30

assert all(w // d == KEYS_BACK for w, d in ATTN_GROUPS)


def _cparams(n_grid):
    return pltpu.CompilerParams(dimension_semantics=("parallel",) * n_grid,
                                vmem_limit_bytes=VMEM_LIMIT_BYTES)


def _resident(shape, layer=None):
    nd = len(shape)
    if layer is None:
        return pl.BlockSpec(shape, lambda *_: (0,) * nd, pipeline_mode=pl.Buffered(1))
    return pl.BlockSpec((None,) + tuple(shape), lambda *_: (layer,) + (0,) * nd, pipeline_mode=pl.Buffered(1))


def _rmsnorm(x, g):
    return x * lax.rsqrt(jnp.mean(x * x, axis=-1, keepdims=True) + RMS_EPS) * g


def _head_lane_masks(rows):
    lane = lax.broadcasted_iota(jnp.int32, (rows, GROUP_WIDTH), 1)
    return [(lane >= h * HEAD_DIM) & (lane < (h + 1) * HEAD_DIM) for h in range(HEADS_PER_GROUP)]


def _rope_tables(pos):
    inv = ROPE_THETA ** (-jnp.arange(0, ROT_DIM, 2, dtype=jnp.float32) / ROT_DIM)
    ang = pos.astype(jnp.float32)[:, None] * inv[None, :]
    cos, sin = jnp.cos(ang), jnp.sin(ang)
    t = pos.shape[0]
    ones = jnp.ones((t, HEAD_DIM - ROT_DIM), jnp.float32)
    zeros = jnp.zeros((t, HEAD_DIM - ROT_DIM), jnp.float32)
    z8 = jnp.zeros((t, ROT_HALF), jnp.float32)
    cos_h = jnp.concatenate([cos, cos, ones], axis=1)
    up_h = jnp.concatenate([z8, sin, zeros], axis=1)
    dn_h = jnp.concatenate([-sin, z8, zeros], axis=1)
    rep = LANES // HEAD_DIM
    return jnp.tile(cos_h, (1, rep)), jnp.tile(up_h, (1, rep)), jnp.tile(dn_h, (1, rep))


def _inproj_kernel(deinterleave, x_ref, g_ref, w_ref, cos_ref, up_ref, dn_ref, *rest):
    if deinterleave:
        qd, kvd = rest[:N_GROUPS], rest[N_GROUPS:2 * N_GROUPS]
        kv_ref, u_ref, slabs = rest[2 * N_GROUPS:]
    else:
        q_ref, kv_ref, u_ref = rest
    tm = x_ref.shape[-2]
    h = _rmsnorm(x_ref[...].reshape(tm, D_MODEL), g_ref[...]).astype(jnp.bfloat16)
    cos, up, dn = cos_ref[...], up_ref[...], dn_ref[...]

    def rope(y):
        parts = []
        for c in range(GROUP_WIDTH // LANES):
            yc = y[:, c * LANES:(c + 1) * LANES]
            parts.append(yc * cos + pltpu.roll(yc, ROT_HALF, 1) * up
                         + pltpu.roll(yc, LANES - ROT_HALF, 1) * dn)
        return jnp.concatenate(parts, axis=1)

    def proj(c0):
        return jnp.dot(h, w_ref[:, c0:c0 + GROUP_WIDTH], preferred_element_type=jnp.float32)

    slab = [0]

    def scatter_by_residue(y, out_ref, lane0, d):
        for c in range(GROUP_WIDTH // LANES):
            s = slab[0]
            slab[0] += 1
            slabs[s] = y[:, c * LANES:(c + 1) * LANES]
            for r in range(d):
                out_ref[0, r, :, lane0 + c * LANES:lane0 + (c + 1) * LANES] = (
                    slabs[s, pl.ds(r, tm // d, stride=d), :].astype(out_ref.dtype))

    for g, (_, d) in enumerate(ATTN_GROUPS):
        yq = rope(proj(g * GROUP_WIDTH)) * (HEAD_DIM ** -0.5)
        yk = rope(proj(ATTN_WIDTH + g * GROUP_WIDTH))
        yv = proj(2 * ATTN_WIDTH + g * GROUP_WIDTH)
        c0 = g * KV_GROUP_WIDTH
        kv_ref[..., c0:c0 + GROUP_WIDTH] = yk.reshape(kv_ref.shape[:-1] + (GROUP_WIDTH,))
        kv_ref[..., c0 + GROUP_WIDTH:c0 + KV_GROUP_WIDTH] = yv.reshape(kv_ref.shape[:-1] + (GROUP_WIDTH,))
        if not deinterleave:
            q_ref[:, g * GROUP_WIDTH:(g + 1) * GROUP_WIDTH] = yq.astype(q_ref.dtype)
        elif d == 1:
            qd[g][0, 0] = yq.astype(qd[g].dtype)
            kvd[g][0, 0, :, :GROUP_WIDTH] = yk.astype(kvd[g].dtype)
            kvd[g][0, 0, :, GROUP_WIDTH:] = yv.astype(kvd[g].dtype)
        else:
            scatter_by_residue(yq, qd[g], 0, d)
            scatter_by_residue(yk, kvd[g], 0, d)
            scatter_by_residue(yv, kvd[g], GROUP_WIDTH, d)
    for c0 in range(0, POOL_WIDTH, GROUP_WIDTH):
        y = proj(3 * ATTN_WIDTH + c0)
        u_ref[..., c0:c0 + GROUP_WIDTH] = y.reshape(u_ref.shape[:-1] + (GROUP_WIDTH,))


def _inproj_prompt(layer, x, norm1, w_in, tables):
    batch, seq, _ = x.shape
    tm = TOKEN_TILE
    assert seq % tm == 0 and tm % (MAX_DILATION * 2 * SUBLANES) == 0
    tok = lambda w: pl.BlockSpec((1, tm, w), lambda b, i: (b, i, 0))
    tab = pl.BlockSpec((tm, LANES), lambda b, i: (i, 0))
    dspec = lambda d, w: pl.BlockSpec((1, d, tm // d, w), lambda b, i: (b, 0, i, 0))
    dshape = lambda d, w: jax.ShapeDtypeStruct((batch, d, seq // d, w), jnp.bfloat16)
    dils = [d for _, d in ATTN_GROUPS]
    n_slabs = sum(3 * (GROUP_WIDTH // LANES) for d in dils if d > 1)
    return pl.pallas_call(
        functools.partial(_inproj_kernel, True),
        grid=(batch, seq // tm),
        in_specs=[tok(D_MODEL), _resident((1, D_MODEL), layer), _resident((D_MODEL, QKVU_WIDTH), layer),
                  tab, tab, tab],
        out_specs=([dspec(d, GROUP_WIDTH) for d in dils] + [dspec(d, KV_GROUP_WIDTH) for d in dils]
                   + [tok(KV_WIDTH), tok(POOL_WIDTH)]),
        out_shape=([dshape(d, GROUP_WIDTH) for d in dils] + [dshape(d, KV_GROUP_WIDTH) for d in dils]
                   + [jax.ShapeDtypeStruct((batch, seq, KV_WIDTH), jnp.float32),
                      jax.ShapeDtypeStruct((batch, seq, POOL_WIDTH), jnp.float32)]),
        scratch_shapes=[pltpu.VMEM((n_slabs, tm, LANES), jnp.float32)],
        compiler_params=_cparams(2),
        name="inproj_prompt",
    )(x, norm1, w_in, *tables)


def _inproj_sample(layer, x, norm1, w_in, tables):
    n = x.shape[0]
    full = lambda w: pl.BlockSpec((n, w), lambda i: (0, 0))
    return pl.pallas_call(
        functools.partial(_inproj_kernel, False),
        grid=(1,),
        in_specs=[full(D_MODEL), _resident((1, D_MODEL), layer), _resident((D_MODEL, QKVU_WIDTH), layer),
                  full(LANES), full(LANES), full(LANES)],
        out_specs=[full(ATTN_WIDTH), full(KV_WIDTH), full(POOL_WIDTH)],
        out_shape=[jax.ShapeDtypeStruct((n, ATTN_WIDTH), jnp.float32),
                   jax.ShapeDtypeStruct((n, KV_WIDTH), jnp.float32),
                   jax.ShapeDtypeStruct((n, POOL_WIDTH), jnp.float32)],
        compiler_params=_cparams(1),
        name="inproj_sample",
    )(x, norm1, w_in, *tables)


def _stacked_head_attention(q, scores_fn, values_fn, bias, masks):
    r = q.shape[0]
    zero = jnp.zeros_like(q)
    q4 = jnp.concatenate([jnp.where(m, q, zero) for m in masks], axis=0).astype(jnp.bfloat16)
    s = scores_fn(q4) + bias
    m = jnp.max(s, axis=-1, keepdims=True)
    p = jnp.exp(s - m)
    den = jnp.sum(p, axis=-1, keepdims=True)
    o4 = values_fn(p.astype(jnp.bfloat16)) * (1.0 / den)
    lse4 = m + jnp.log(den)
    o = jnp.zeros((r, GROUP_WIDTH), jnp.float32)
    lse = jnp.zeros((r, GROUP_WIDTH), jnp.float32)
    for h, mk in enumerate(masks):
        o = jnp.where(mk, o4[h * r:(h + 1) * r], o)
        lse = jnp.where(mk, lse4[h * r:(h + 1) * r], lse)
    return o, lse


def _merge_groups(os, lses):
    mx = lses[0]
    for l in lses[1:]:
        mx = jnp.maximum(mx, l)
    es = [jnp.exp(l - mx) for l in lses]
    num, den = es[0] * os[0], es[0]
    for e, o in zip(es[1:], os[1:]):
        num, den = num + e * o, den + e
    return num / den


_dot_nt = functools.partial(lax.dot_general, dimension_numbers=(((1,), (1,)), ((), ())),
                            preferred_element_type=jnp.float32)
_dot = functools.partial(jnp.dot, preferred_element_type=jnp.float32)


def _prompt_attn_kernel(q0_ref, q1_ref, q2_ref, kv0_ref, h0_ref, kv1_ref, h1_ref, kv2_ref, h2_ref, o_ref,
                        kvs0, kvs1, bias_scr, onat, lnat):
    first_super = pl.program_id(1) == 0
    masks = _head_lane_masks(Q_BLOCK)
    row = lax.broadcasted_iota(jnp.int32, (Q_BLOCK, 2 * Q_BLOCK), 0)
    col = lax.broadcasted_iota(jnp.int32, (Q_BLOCK, 2 * Q_BLOCK), 1)
    rel = col - row
    band = jnp.where((rel >= 0) & (rel <= KEYS_BACK), 0.0, NEG_BIG).astype(jnp.float32)
    no_halo = band + jnp.where(col < Q_BLOCK, NEG_BIG, 0.0).astype(jnp.float32)
    bias_scr[0] = jnp.concatenate([band] * HEADS_PER_GROUP, axis=0)
    bias_scr[1] = jnp.concatenate([no_halo] * HEADS_PER_GROUP, axis=0)
    d1 = ATTN_GROUPS[1][1]
    kvs0[0:Q_BLOCK] = h0_ref[0, 0]
    kvs0[Q_BLOCK:] = kv0_ref[0, 0]
    for r in range(d1):
        kvs1[r, 0:Q_BLOCK] = h1_ref[0, r]
        kvs1[r, Q_BLOCK:] = kv1_ref[0, r]

    def block(q, kv, first, group, start, stride):
        bias = bias_scr[jnp.where(first, 1, 0)]
        k, v = kv[:, :GROUP_WIDTH], kv[:, GROUP_WIDTH:]
        o, lse = _stacked_head_attention(q, lambda q4: _dot_nt(q4, k), lambda p: _dot(p, v), bias, masks)
        for c in range(GROUP_WIDTH // LANES):
            rows = pl.ds(start, Q_BLOCK) if stride == 1 else pl.ds(start, Q_BLOCK, stride=stride)
            onat[group, c, rows, :] = o[:, c * LANES:(c + 1) * LANES]
            lnat[group, c, rows, :] = lse[:, c * LANES:(c + 1) * LANES]

    def g0_body(i, carry):
        m0 = pl.multiple_of(i * Q_BLOCK, Q_BLOCK)
        block(q0_ref[0, 0, pl.ds(m0, Q_BLOCK), :], kvs0[pl.ds(m0, 2 * Q_BLOCK), :],
              first_super & (i == 0), 0, m0, 1)
        return carry

    blocks_per_res1 = BLOCKS_PER_SUPER // d1

    def g1_body(i, carry):
        r, c = i // blocks_per_res1, i % blocks_per_res1
        m0 = pl.multiple_of(c * Q_BLOCK, Q_BLOCK)
        block(q1_ref[0, r, pl.ds(m0, Q_BLOCK), :], kvs1[r, pl.ds(m0, 2 * Q_BLOCK), :],
              first_super & (c == 0), 1, r + d1 * m0, d1)
        return carry

    def g2_body(r, carry):
        kv = jnp.concatenate([h2_ref[0, r], kv2_ref[0, r]], axis=0)
        block(q2_ref[0, r], kv, first_super, 2, r, MAX_DILATION)
        return carry

    lax.fori_loop(0, BLOCKS_PER_SUPER, g0_body, 0, unroll=4)
    lax.fori_loop(0, BLOCKS_PER_SUPER, g1_body, 0, unroll=4)
    lax.fori_loop(0, BLOCKS_PER_SUPER, g2_body, 0, unroll=4)

    def merge_body(j, carry):
        r0 = pl.multiple_of(j * MERGE_ROWS, MERGE_ROWS)
        for c in range(GROUP_WIDTH // LANES):
            os = [onat[g, c, pl.ds(r0, MERGE_ROWS), :] for g in range(N_GROUPS)]
            ls = [lnat[g, c, pl.ds(r0, MERGE_ROWS), :] for g in range(N_GROUPS)]
            o_ref[0, pl.ds(r0, MERGE_ROWS), c * LANES:(c + 1) * LANES] = _merge_groups(os, ls).astype(o_ref.dtype)
        return carry

    lax.fori_loop(0, SUPER_TILE // MERGE_ROWS, merge_body, 0)


def _prompt_attention(qd, kvd, batch, seq):
    assert seq % SUPER_TILE == 0 and [d for _, d in ATTN_GROUPS] == [1, 4, 16]
    in_specs, operands = [], []
    for g, (_, d) in enumerate(ATTN_GROUPS):
        in_specs.append(pl.BlockSpec((1, d, SUPER_TILE // d, GROUP_WIDTH), lambda b, s: (b, 0, s, 0)))
        operands.append(qd[g])
    for g, (_, d) in enumerate(ATTN_GROUPS):
        per_super = BLOCKS_PER_SUPER // d
        in_specs.append(pl.BlockSpec((1, d, SUPER_TILE // d, KV_GROUP_WIDTH), lambda b, s: (b, 0, s, 0)))
        in_specs.append(pl.BlockSpec((1, d, Q_BLOCK, KV_GROUP_WIDTH),
                                     lambda b, s, per_super=per_super: (b, 0, jnp.maximum(s * per_super - 1, 0), 0)))
        operands += [kvd[g], kvd[g]]
    d1 = ATTN_GROUPS[1][1]
    slabs = GROUP_WIDTH // LANES
    return pl.pallas_call(
        _prompt_attn_kernel,
        grid=(batch, seq // SUPER_TILE),
        in_specs=in_specs,
        out_specs=pl.BlockSpec((1, SUPER_TILE, GROUP_WIDTH), lambda b, s: (b, s, 0)),
        out_shape=jax.ShapeDtypeStruct((batch, seq, GROUP_WIDTH), jnp.bfloat16),
        scratch_shapes=[pltpu.VMEM((SUPER_TILE + Q_BLOCK, KV_GROUP_WIDTH), jnp.bfloat16),
                        pltpu.VMEM((d1, SUPER_TILE // d1 + Q_BLOCK, KV_GROUP_WIDTH), jnp.bfloat16),
                        pltpu.VMEM((2, HEADS_PER_GROUP * Q_BLOCK, 2 * Q_BLOCK), jnp.float32),
                        pltpu.VMEM((N_GROUPS, slabs, SUPER_TILE, LANES), jnp.float32),
                        pltpu.VMEM((N_GROUPS, slabs, SUPER_TILE, LANES), jnp.float32)],
        compiler_params=_cparams(2),
        name="prompt_attn",
    )(*operands)


def _sample_bias(g, dec_seq):
    w, d = ATTN_GROUPS[g]
    cache_len = w
    t = np.arange(dec_seq)[:, None]
    diff_c = t + cache_len - np.arange(cache_len)[None, :]
    ok_c = (diff_c >= 0) & (diff_c % d == 0) & (diff_c <= w)
    new_t = np.arange(LANES)[None, :] - (LANES - dec_seq)
    diff_new = t - new_t
    ok_new = (new_t >= 0) & (diff_new >= 0) & (diff_new % d == 0) & (diff_new <= w)
    bias = np.where(np.concatenate([ok_c, ok_new], axis=1), 0.0, NEG_BIG).astype(np.float32)
    return np.tile(bias, (HEADS_PER_GROUP, 1))


def _sample_attn_kernel(dec_seq, q_ref, new_ref, c0_ref, c1_ref, c2_ref, b0_ref, b1_ref, b2_ref, o_ref):
    masks = _head_lane_masks(dec_seq)
    q = q_ref[0]
    os, lses = [], []
    for g, (c_ref, b_ref) in enumerate(((c0_ref, b0_ref), (c1_ref, b1_ref), (c2_ref, b2_ref))):
        kvt = jnp.concatenate([c_ref[0, 0], new_ref[0, g]], axis=1).astype(jnp.bfloat16)
        kt, vt = kvt[:GROUP_WIDTH], kvt[GROUP_WIDTH:]
        o, lse = _stacked_head_attention(q[:, g * GROUP_WIDTH:(g + 1) * GROUP_WIDTH],
                                         lambda q4, kt=kt: _dot(q4, kt), lambda p, vt=vt: _dot_nt(p, vt),
                                         b_ref[...], masks)
        os.append(o)
        lses.append(lse)
    o_ref[0] = _merge_groups(os, lses).astype(o_ref.dtype)


def _sample_attention(layer, q, new_t, caches_t, dec_batch, dec_seq):
    biases = [jnp.asarray(_sample_bias(g, dec_seq)) for g in range(N_GROUPS)]
    per_b = lambda shape: pl.BlockSpec((1,) + shape, lambda b: (b,) + (0,) * len(shape))
    cache_spec = lambda c: pl.BlockSpec((1, 1) + c.shape[2:], lambda b: (layer, b, 0, 0))
    return pl.pallas_call(
        functools.partial(_sample_attn_kernel, dec_seq),
        grid=(dec_batch,),
        in_specs=[per_b((dec_seq, ATTN_WIDTH)), per_b((N_GROUPS, KV_GROUP_WIDTH, LANES))]
        + [cache_spec(c) for c in caches_t] + [_resident(b.shape) for b in biases],
        out_specs=per_b((dec_seq, GROUP_WIDTH)),
        out_shape=jax.ShapeDtypeStruct((dec_batch, dec_seq, GROUP_WIDTH), jnp.float32),
        compiler_params=_cparams(1),
        name="sample_attn",
    )(q, new_t, *caches_t, *biases)


def _pool_counts(shape, axis, pos0, w):
    pos = pos0 + lax.broadcasted_iota(jnp.int32, shape, axis)
    return jnp.minimum(w, pos + 1).astype(jnp.float32)


def _sample_pool_kernel(dec_seq, state_ref, u_ref, p_ref):
    ext = jnp.concatenate([state_ref[0], u_ref[...]], axis=0)
    for gi, w in enumerate(POOL_WINDOWS):
        sl = slice(gi * POOL_GROUP_DIM, (gi + 1) * POOL_GROUP_DIM)
        s = ext[POOL_PAD:POOL_PAD + dec_seq, :, sl]
        for k in range(1, w):
            s = s + ext[POOL_PAD - k:POOL_PAD - k + dec_seq, :, sl]
        cnt = _pool_counts((dec_seq, 1, 1), 0, PAST_LEN, w)
        p_ref[:, :, sl] = s / cnt - u_ref[:, :, sl]


def _sample_pool(layer, state_t, u_t):
    dec_seq, dec_batch, _ = u_t.shape
    return pl.pallas_call(
        functools.partial(_sample_pool_kernel, dec_seq),
        grid=(1,),
        in_specs=[pl.BlockSpec((1,) + state_t.shape[1:], lambda i: (layer, 0, 0, 0)),
                  pl.BlockSpec(u_t.shape, lambda i: (0, 0, 0))],
        out_specs=pl.BlockSpec(u_t.shape, lambda i: (0, 0, 0)),
        out_shape=jax.ShapeDtypeStruct(u_t.shape, jnp.float32),
        compiler_params=_cparams(1),
        name="sample_pool",
    )(state_t, u_t)


POOL_HALO = 2 * SUBLANES
POOL_LEAD = 3 * SUBLANES


def _pool_features_tile(u_ref, halo_ref, buf_a, buf_b, tile_in_seq, tm):
    zeros8 = jnp.zeros((SUBLANES, POOL_WIDTH), jnp.float32)
    buf_a[0:SUBLANES, :] = zeros8
    buf_b[0:SUBLANES, :] = zeros8
    buf_a[SUBLANES:POOL_LEAD, :] = jnp.where(tile_in_seq == 0, 0.0, halo_ref[...])
    buf_a[POOL_LEAD:, :] = u_ref[...]
    n = tm + POOL_HALO
    src, dst = buf_a, buf_b
    for level, shift in enumerate((1, 2, 4, 8)):
        lanes = slice(level * POOL_GROUP_DIM, POOL_WIDTH)
        dst[SUBLANES:SUBLANES + n, lanes] = (src[SUBLANES:SUBLANES + n, lanes]
                                             + src[SUBLANES - shift:SUBLANES - shift + n, lanes])
        src, dst = dst, src
    feats = []
    for gi, w in enumerate(POOL_WINDOWS):
        sl = slice(gi * POOL_GROUP_DIM, (gi + 1) * POOL_GROUP_DIM)
        holder = buf_b if gi % 2 == 0 else buf_a
        s = holder[POOL_LEAD:POOL_LEAD + tm, sl]
        feats.append(s / _pool_counts((tm, 1), 0, tile_in_seq * tm, w) - u_ref[:, sl])
    return feats


def _post_kernel(pool_from_halo, final, tiles_per_seq, *refs):
    it = iter(refs)
    x_ref, oa_ref, u_ref = next(it), next(it), next(it)
    halo_ref = next(it) if pool_from_halo else None
    (n1_ref, wg_ref, wpa_ref, wpb_ref, lin_ref, ps_ref, wo_ref, n2_ref, wup_ref, wdn_ref) = [next(it) for _ in range(10)]
    fn_ref = next(it) if final else None
    out_ref = next(it)
    bufs = list(it)
    tm = x_ref.shape[0]
    bf = jnp.bfloat16

    x = x_ref[...]
    if pool_from_halo:
        feats = _pool_features_tile(u_ref, halo_ref, bufs[0], bufs[1], pl.program_id(0) % tiles_per_seq, tm)
    else:
        feats = [u_ref[:, gi * POOL_GROUP_DIM:(gi + 1) * POOL_GROUP_DIM] for gi in range(len(POOL_WINDOWS))]
    o_b = jnp.concatenate([_dot(f.astype(bf), lin_ref[gi]) for gi, f in enumerate(feats)], axis=1)
    o_b = (o_b * ps_ref[...]).astype(bf)
    h = _rmsnorm(x, n1_ref[...]).astype(bf)
    gates = 1.0 / (1.0 + jnp.exp(-_dot(h, wg_ref[:, QKVU_WIDTH:])))
    mixed = (gates[:, :D_MODEL] * _dot(oa_ref[...].astype(bf), wpa_ref[...])
             + gates[:, D_MODEL:] * _dot(o_b, wpb_ref[...]))
    x = x + _dot(mixed.astype(bf), wo_ref[...])
    h2 = _rmsnorm(x, n2_ref[...]).astype(bf)
    act = jnp.square(jnp.maximum(_dot(h2, wup_ref[...]), 0.0)).astype(bf)
    x = x + _dot(act, wdn_ref[...])
    if final:
        x = _rmsnorm(x, fn_ref[...])
    out_ref[...] = x


def _post(layer, x, o_a, u_or_feats, weights, final_norm, rows_per_seq, pool_from_halo):
    n = x.shape[0]
    tm = min(TOKEN_TILE, n)
    assert n % tm == 0 and tm % POOL_HALO == 0
    tiles_per_seq = max(rows_per_seq // tm, 1)
    final = final_norm is not None
    tok = lambda w: pl.BlockSpec((tm, w), lambda i: (i, 0))
    in_specs = [tok(D_MODEL), tok(GROUP_WIDTH), tok(POOL_WIDTH)]
    operands = [x, o_a, u_or_feats]
    scratch = []
    if pool_from_halo:
        assert rows_per_seq % tm == 0
        ratio = tm // POOL_HALO
        in_specs.append(pl.BlockSpec((POOL_HALO, POOL_WIDTH), lambda i: (jnp.maximum(i * ratio - 1, 0), 0)))
        operands.append(u_or_feats)
        scratch = [pltpu.VMEM((POOL_LEAD + tm, POOL_WIDTH), jnp.float32)] * 2
    in_specs += [_resident(w.shape[1:], layer) for w in weights]
    operands += list(weights)
    if final:
        in_specs.append(_resident(final_norm.shape))
        operands.append(final_norm)
    return pl.pallas_call(
        functools.partial(_post_kernel, pool_from_halo, final, tiles_per_seq),
        grid=(n // tm,),
        in_specs=in_specs,
        out_specs=tok(D_MODEL),
        out_shape=jax.ShapeDtypeStruct((n, D_MODEL), jnp.float32),
        scratch_shapes=scratch,
        compiler_params=_cparams(1),
        name="post_final" if final else "post",
    )(*operands)


def _cache_update_kernel(dec_seq, new_ref, c0_ref, c1_ref, c2_ref, o0_ref, o1_ref, o2_ref):
    lane = lax.broadcasted_iota(jnp.int32, (KV_GROUP_WIDTH, LANES), 1)
    for g, (c_ref, o_ref) in enumerate(((c0_ref, o0_ref), (c1_ref, o1_ref), (c2_ref, o2_ref))):
        length = c_ref.shape[-1]
        shifted = pltpu.roll(c_ref[0, 0], length - dec_seq, 1)
        if length > LANES:
            o_ref[0, 0, :, :length - LANES] = shifted[:, :length - LANES]
        o_ref[0, 0, :, length - LANES:] = jnp.where(lane < LANES - dec_seq, shifted[:, length - LANES:],
                                                    new_ref[0, 0, g])


def _cache_update(new_t, caches_t):
    depth, dec_batch = new_t.shape[:2]
    dec_seq = SUBLANES
    spec = lambda c: pl.BlockSpec((1, 1) + c.shape[2:], lambda l, b: (l, b) + (0,) * (c.ndim - 2))
    return pl.pallas_call(
        functools.partial(_cache_update_kernel, dec_seq),
        grid=(depth, dec_batch),
        in_specs=[spec(new_t)] + [spec(c) for c in caches_t],
        out_specs=[spec(c) for c in caches_t],
        out_shape=[jax.ShapeDtypeStruct(c.shape, c.dtype) for c in caches_t],
        compiler_params=_cparams(2),
        name="cache_update",
    )(new_t, *caches_t)


def kernel(x_prompt, x_sample, cache_kv_w128, cache_kv_w512, cache_kv_w2048, state_pool, norm1, w_in, w_pa, w_pb,
           pool_lin, pool_scale, w_o, norm2, w_up, w_down, final_norm):
    batch, seq, _ = x_prompt.shape
    dec_batch, dec_seq, _ = x_sample.shape
    depth = w_in.shape[0]
    assert dec_seq == SUBLANES
    bf = jnp.bfloat16
    caches = (cache_kv_w128, cache_kv_w512, cache_kv_w2048)
    caches_t = [jnp.transpose(c, (0, 1, 3, 4, 5, 2)).reshape(depth, dec_batch, KV_GROUP_WIDTH, c.shape[2])
                for c in caches]
    state_t = jnp.transpose(state_pool, (0, 2, 1, 3))

    w_in_b, w_pa_b, w_pb_b, lin_b, w_o_b, w_up_b, w_dn_b = (
        w.astype(bf) for w in (w_in, w_pa, w_pb, pool_lin, w_o, w_up, w_down))
    norm1_r, norm2_r, scale_r = norm1[:, None, :], norm2[:, None, :], pool_scale[:, None, :]
    weights = (norm1_r, w_in_b, w_pa_b, w_pb_b, lin_b, scale_r, w_o_b, norm2_r, w_up_b, w_dn_b)
    fnorm = final_norm[None, :]

    tab_p = _rope_tables(jnp.arange(seq, dtype=jnp.int32))
    tab_s = tuple(jnp.tile(t, (dec_batch, 1)) for t in _rope_tables(PAST_LEN + jnp.arange(dec_seq, dtype=jnp.int32)))

    xp = x_prompt
    xs = x_sample.reshape(dec_batch * dec_seq, D_MODEL)
    n_p, n_s = batch * seq, dec_batch * dec_seq
    kv_p, u_p, new_ts, u_s = [], [], [], []
    for l in range(depth):
        fin = fnorm if l == depth - 1 else None
        res = _inproj_prompt(l, xp, norm1_r, w_in_b, tab_p)
        qd, kvd, kv, u = res[:N_GROUPS], res[N_GROUPS:2 * N_GROUPS], res[-2], res[-1]
        o_a = _prompt_attention(qd, kvd, batch, seq)
        xp = _post(l, xp.reshape(n_p, D_MODEL), o_a.reshape(n_p, GROUP_WIDTH), u.reshape(n_p, POOL_WIDTH), weights,
                   fin, seq, True).reshape(batch, seq, D_MODEL)
        kv_p.append([kv[:, seq - min(w, seq):, g * KV_GROUP_WIDTH:(g + 1) * KV_GROUP_WIDTH]
                     for g, (w, _) in enumerate(ATTN_GROUPS)])
        u_p.append(u[:, seq - POOL_PAD:])
        q, kv, u = _inproj_sample(l, xs, norm1_r, w_in_b, tab_s)
        new_t = jnp.transpose(kv.reshape(dec_batch, dec_seq, N_GROUPS, KV_GROUP_WIDTH), (0, 2, 3, 1))
        new_t = jnp.pad(new_t, ((0, 0), (0, 0), (0, 0), (LANES - dec_seq, 0)))
        o_a = _sample_attention(l, q.reshape(dec_batch, dec_seq, ATTN_WIDTH), new_t, caches_t, dec_batch, dec_seq)
        u3 = u.reshape(dec_batch, dec_seq, POOL_WIDTH)
        feats = _sample_pool(l, state_t, jnp.transpose(u3, (1, 0, 2)))
        feats = jnp.transpose(feats, (1, 0, 2)).reshape(n_s, POOL_WIDTH)
        xs = _post(l, xs, o_a.reshape(n_s, GROUP_WIDTH), feats, weights, fin, n_s, False)
        new_ts.append(new_t)
        u_s.append(u3)

    y_prompt = xp
    y_sample = xs.reshape(dec_batch, dec_seq, D_MODEL)
    prompt_kv = []
    for g, (w, _) in enumerate(ATTN_GROUPS):
        sl = jnp.stack([per_layer[g] for per_layer in kv_p], axis=0)
        prompt_kv.append(sl.reshape(depth, batch, min(w, seq), 2, HEADS_PER_GROUP, HEAD_DIM))
    pool_prompt = jnp.stack(u_p, axis=0)
    updated = _cache_update(jnp.stack(new_ts, axis=0), caches_t)
    sample_kv = [jnp.transpose(c.reshape(depth, dec_batch, 2, HEADS_PER_GROUP, HEAD_DIM, c.shape[-1]),
                               (0, 1, 5, 2, 3, 4)) for c in updated]
    pool_sample = jnp.concatenate([state_pool, jnp.stack(u_s, axis=0)], axis=2)[:, :, -POOL_PAD:]
    return (y_prompt, y_sample, *prompt_kv, pool_prompt, *sample_kv, pool_sample)
```
